```python
import jax, jax.numpy as jnp
from jax import lax
import numpy as np


D_MODEL = 4096
BATCH = 2
SEQ = 8192
DEPTH = 2

CTX_LEN = 256
GRID_W = 64

N_HEADS = 16
Q_LORA = 1024
KV_LORA = 512
QK_NOPE = 128
QK_ROPE = 64
V_HEAD = 128
QK_HEAD = QK_NOPE + QK_ROPE
MLA_WIDTH = N_HEADS * V_HEAD
MLA_IN = Q_LORA + KV_LORA + QK_ROPE
SOFTMAX_SCALE = QK_HEAD ** -0.5
ROPE_BASE = 10000.0
Q_BLOCK = 128

CONV_WIDTH = D_MODEL // 2
CONV_KERNEL = 31
CONV_PAD = CONV_KERNEL // 2

N_BRANCH = 2
IN_COLS = MLA_IN + 2 * CONV_WIDTH + N_BRANCH * D_MODEL

FF_DENSE = 11008
N_EXPERTS = 8
TOP_K = 2
FF_EXPERT = 3584

EPS = 1e-6

kernel_name = 'hybrid_mla_conformer_moe_dit_block'


def rms_norm(x, g):
    xf = x.astype(jnp.float32)
    y = xf * lax.rsqrt(jnp.mean(xf * xf, axis=-1, keepdims=True) + EPS)
    return (y * g.astype(jnp.float32)).astype(x.dtype)


def layer_norm(x, g, b):
    xf = x.astype(jnp.float32)
    mu = jnp.mean(xf, axis=-1, keepdims=True)
    var = jnp.mean(jnp.square(xf - mu), axis=-1, keepdims=True)
    y = (xf - mu) * lax.rsqrt(var + EPS) * g.astype(jnp.float32) + b.astype(jnp.float32)
    return y.astype(x.dtype)


def modulate(h, shift, scale):
    return h * (1 + scale) + shift


def axial_rope_tables(n_tokens):
    n_rows = n_tokens // GRID_W
    rows = jnp.repeat(jnp.arange(n_rows), GRID_W).astype(jnp.float32)
    cols = jnp.tile(jnp.arange(GRID_W), n_rows).astype(jnp.float32)
    quarter = QK_ROPE // 4
    inv = ROPE_BASE ** (-jnp.arange(quarter, dtype=jnp.float32) / quarter)
    ang = jnp.stack([rows[:, None] * inv, cols[:, None] * inv], axis=1)
    return jnp.cos(ang)[:, None], jnp.sin(ang)[:, None]


def apply_axial_rope(t, cos, sin):
    B, L, H, _ = t.shape
    tr = t[..., QK_NOPE:].reshape(B, L, H, 2, 2, QK_ROPE // 4)
    t1, t2 = tr[..., 0, :], tr[..., 1, :]
    r = jnp.stack([t1 * cos - t2 * sin, t1 * sin + t2 * cos], axis=-2)
    r = r.reshape(B, L, H, QK_ROPE).astype(t.dtype)
    return jnp.concatenate([t[..., :QK_NOPE], r], axis=-1)


def mla_qkv(proj, q_lora_g, w_uq, kv_lora_g, w_ukv, q_norm_g, k_norm_g, rope):
    B, L, _ = proj.shape
    c_q = proj[..., :Q_LORA]
    c_kv = proj[..., Q_LORA:Q_LORA + KV_LORA]
    k_r = proj[..., Q_LORA + KV_LORA:]
    q = (rms_norm(c_q, q_lora_g) @ w_uq).reshape(B, L, N_HEADS, QK_HEAD)
    kv = (rms_norm(c_kv, kv_lora_g) @ w_ukv).reshape(B, L, N_HEADS, QK_NOPE + V_HEAD)
    k_nope, v = kv[..., :QK_NOPE], kv[..., QK_NOPE:]
    k = jnp.concatenate([k_nope, jnp.broadcast_to(k_r[:, :, None, :], (B, L, N_HEADS, QK_ROPE))], axis=-1)
    q = rms_norm(q, q_norm_g)
    k = rms_norm(k, k_norm_g)
    if rope is not None:
        cos, sin = rope
        q = apply_axial_rope(q, cos, sin)
        k = apply_axial_rope(k, cos, sin)
    return q, k, v


def attend(q, k, v):
    B, L, H, _ = q.shape
    nb = L // Q_BLOCK
    qb = jnp.moveaxis(q.reshape(B, nb, Q_BLOCK, H, QK_HEAD), 1, 0)

    def block(qi):
        s = jnp.einsum('bqhd,bkhd->bhqk', qi, k).astype(jnp.float32) * SOFTMAX_SCALE
        p = jax.nn.softmax(s, axis=-1).astype(v.dtype)
        return jnp.einsum('bhqk,bkhd->bqhd', p, v)

    o = lax.map(block, qb)
    return jnp.moveaxis(o, 0, 1).reshape(B, L, H * V_HEAD)


def depthwise_conv(u, w, b):
    out = lax.conv_general_dilated(
        u, w[:, None, :].astype(u.dtype), window_strides=(1,), padding=[(CONV_PAD, CONV_PAD)],
        dimension_numbers=('NWC', 'WIO', 'NWC'), feature_group_count=u.shape[-1])
    return out + b


def conformer_conv(glu_in, w_dw, b_dw, ln_g, ln_b):
    a, g = jnp.split(glu_in, 2, axis=-1)
    u = a * jax.nn.sigmoid(g)
    u = depthwise_conv(u, w_dw, b_dw)
    u = layer_norm(u, ln_g, ln_b)
    return jax.nn.silu(u)


def merge_branches(proj, attn, w_oa, w_dw, b_dw, ln_g, ln_b, w_ob, b_gate, w_out):
    glu_in = proj[..., MLA_IN:MLA_IN + 2 * CONV_WIDTH]
    gates = jax.nn.sigmoid(proj[..., MLA_IN + 2 * CONV_WIDTH:] + b_gate)
    g_a, g_b = jnp.split(gates, 2, axis=-1)
    y_a = attn @ w_oa
    y_b = conformer_conv(glu_in, w_dw, b_dw, ln_g, ln_b) @ w_ob
    return (g_a * y_a + g_b * y_b) @ w_out


def swiglu(h, w_gate, w_up, w_down):
    return (jax.nn.silu(h @ w_gate) * (h @ w_up)) @ w_down


def moe_swiglu(h, w_router, b_router, w_e_gate, w_e_up, w_e_down):
    logits = jnp.einsum('bld,de->ble', h, w_router).astype(jnp.float32) + b_router.astype(jnp.float32)
    probs = jax.nn.softmax(logits, axis=-1)
    top_p, top_i = lax.top_k(probs, TOP_K)
    top_p = top_p / jnp.sum(top_p, axis=-1, keepdims=True)
    combine = jnp.einsum('blk,blke->ble', top_p,
                         jax.nn.one_hot(top_i, N_EXPERTS, dtype=jnp.float32)).astype(h.dtype)
    out = jnp.zeros_like(h)
    for e in range(N_EXPERTS):
        out = out + combine[..., e:e + 1] * swiglu(h, w_e_gate[e], w_e_up[e], w_e_down[e])
    return out


def setup_inputs(seed: int = 0) -> dict:
    key = jax.random.key(seed)
    ks = iter(jax.random.split(key, 40))
    f32 = jnp.float32
    n_dense = (DEPTH + 1) // 2
    n_moe = DEPTH // 2

    def nrm(shape, fan_in):
        return jax.random.normal(next(ks), shape, f32) * (fan_in ** -0.5)

    def gain(shape):
        return 1.0 + 0.05 * jax.random.normal(next(ks), shape, f32)

    def small(shape, s=0.02):
        return s * jax.random.normal(next(ks), shape, f32)

    return {
        'x': jax.random.normal(next(ks), (BATCH, SEQ, D_MODEL), f32),
        'c': jax.random.normal(next(ks), (BATCH, D_MODEL), f32),
        'ctx': jax.random.normal(next(ks), (BATCH, CTX_LEN, D_MODEL), f32),
        'c_ctx': jax.random.normal(next(ks), (D_MODEL,), f32),
        'w_mod': nrm((DEPTH, D_MODEL, 6 * D_MODEL), D_MODEL),
        'b_mod': small((DEPTH, 6 * D_MODEL)),
        'norm1_g': gain((DEPTH, D_MODEL)),
        'norm2_g': gain((DEPTH, D_MODEL)),
        'w_in': nrm((DEPTH, D_MODEL, IN_COLS), D_MODEL),
        'b_gate': small((DEPTH, N_BRANCH * D_MODEL)),
        'q_lora_g': gain((DEPTH, Q_LORA)),
        'w_uq': nrm((DEPTH, Q_LORA, N_HEADS * QK_HEAD), Q_LORA),
        'kv_lora_g': gain((DEPTH, KV_LORA)),
        'w_ukv': nrm((DEPTH, KV_LORA, N_HEADS * (QK_NOPE + V_HEAD)), KV_LORA),
        'q_norm_g': gain((DEPTH, QK_HEAD)),
        'k_norm_g': gain((DEPTH, QK_HEAD)),
        'w_oa': nrm((DEPTH, MLA_WIDTH, D_MODEL), MLA_WIDTH),
        'w_dw': nrm((DEPTH, CONV_KERNEL, CONV_WIDTH), CONV_KERNEL),
        'b_dw': small((DEPTH, CONV_WIDTH)),
        'conv_ln_g': gain((DEPTH, CONV_WIDTH)),
        'conv_ln_b': small((DEPTH, CONV_WIDTH)),
        'w_ob': nrm((DEPTH, CONV_WIDTH, D_MODEL), CONV_WIDTH),
        'w_out': nrm((DEPTH, D_MODEL, D_MODEL), D_MODEL),
        'w_ff_gate': nrm((n_dense, D_MODEL, FF_DENSE), D_MODEL),
        'w_ff_up': nrm((n_dense, D_MODEL, FF_DENSE), D_MODEL),
        'w_ff_down': nrm((n_dense, FF_DENSE, D_MODEL), FF_DENSE),
        'w_router': nrm((n_moe, D_MODEL, N_EXPERTS), D_MODEL),
        'b_router': small((n_moe, N_EXPERTS), 0.01),
        'w_e_gate': nrm((n_moe, N_EXPERTS, D_MODEL, FF_EXPERT), D_MODEL),
        'w_e_up': nrm((n_moe, N_EXPERTS, D_MODEL, FF_EXPERT), D_MODEL),
        'w_e_down': nrm((n_moe, N_EXPERTS, FF_EXPERT, D_MODEL), FF_EXPERT),
    }


def reference(x, c, ctx, c_ctx, w_mod, b_mod, norm1_g, norm2_g, w_in, b_gate,
              q_lora_g, w_uq, kv_lora_g, w_ukv, q_norm_g, k_norm_g, w_oa,
              w_dw, b_dw, conv_ln_g, conv_ln_b, w_ob, w_out,
              w_ff_gate, w_ff_up, w_ff_down, w_router, b_router, w_e_gate, w_e_up, w_e_down):
    rope = axial_rope_tables(x.shape[1])
    xc = ctx

    def channel_mix(h, l):
        i = l // 2
        if l % 2 == 0:
            return swiglu(h, w_ff_gate[i], w_ff_up[i], w_ff_down[i])
        return moe_swiglu(h, w_router[i], b_router[i], w_e_gate[i], w_e_up[i], w_e_down[i])

    for l in range(DEPTH):
        last = l == DEPTH - 1
        mod = jax.nn.silu(c) @ w_mod[l] + b_mod[l]
        sh1, sc1, g1, sh2, sc2, g2 = jnp.split(mod[:, None, :], 6, axis=-1)
        mod_c = jax.nn.silu(c_ctx) @ w_mod[l] + b_mod[l]
        csh1, csc1, cg1, csh2, csc2, cg2 = jnp.split(mod_c, 6)
        mla_p = (q_lora_g[l], w_uq[l], kv_lora_g[l], w_ukv[l], q_norm_g[l], k_norm_g[l])
        branch_p = (w_oa[l], w_dw[l], b_dw[l], conv_ln_g[l], conv_ln_b[l], w_ob[l], b_gate[l], w_out[l])

        hc = modulate(rms_norm(xc, norm1_g[l]), csh1, csc1)
        proj_c = hc @ (w_in[l][:, :MLA_IN] if last else w_in[l])
        qc, kc, vc = mla_qkv(proj_c[..., :MLA_IN], *mla_p, None)

        h = modulate(rms_norm(x, norm1_g[l]), sh1, sc1)
        proj = h @ w_in[l]
        q, k, v = mla_qkv(proj[..., :MLA_IN], *mla_p, rope)
        attn = attend(q, jnp.concatenate([kc, k], axis=1), jnp.concatenate([vc, v], axis=1))
        x = x + g1 * merge_branches(proj, attn, *branch_p)
        x = x + g2 * channel_mix(modulate(rms_norm(x, norm2_g[l]), sh2, sc2), l)

        if not last:
            attn_c = attend(qc, kc, vc)
            xc = xc + cg1 * merge_branches(proj_c, attn_c, *branch_p)
            xc = xc + cg2 * channel_mix(modulate(rms_norm(xc, norm2_g[l]), csh2, csc2), l)
    return x
```

```python
import functools

import jax
import jax.numpy as jnp
from jax import lax
from jax.experimental import pallas as pl
from jax.experimental.pallas import tpu as pltpu

F32 = jnp.float32
BF16 = jnp.bfloat16

QK_NOPE = 128
QK_ROPE = 64
V_HEAD = 128
QK_HEAD = QK_NOPE + QK_ROPE
GRID_W = 64
ROPE_BASE = 10000.0
EPS = 1e-6
SOFTMAX_SCALE = QK_HEAD ** -0.5

LANE = 128
MXU_DIM = 256
VMEM_BYTES_V7X = 64 * 2 ** 20
VMEM_CAP = 56 * 2 ** 20

ROW_TILE = 256
CONV_HALO = 16
NEG_BIG = -1e30


def _vmem_limit(block_bytes, scratch_bytes=0, temp_bytes=0):
    need = 2 * block_bytes + scratch_bytes + 2 * temp_bytes + (8 << 20)
    return int(min(max(need, 16 << 20), VMEM_CAP))


def _cparams(n_axes, vmem):
    return pltpu.CompilerParams(dimension_semantics=("arbitrary",) * n_axes, vmem_limit_bytes=vmem)


def _pick_tile(n, candidates):
    for c in candidates:
        if n % c == 0:
            return c
    return n


def _sigmoid(x):
    return 1.0 / (1.0 + jnp.exp(-x))


def _dot(a, b):
    return jnp.dot(a, b, preferred_element_type=F32)


def _mod_kernel(c_ref, w_ref, b_ref, o_ref):
    c = c_ref[...]
    s = (c * _sigmoid(c)).astype(BF16)
    o_ref[0] = _dot(s, w_ref[0].astype(BF16)) + b_ref[0]


def _mod_call(cin, w_mod, b_mod):
    depth, d, n = w_mod.shape
    tn = _pick_tile(n, (1024, 512, 256, 128))
    return pl.pallas_call(
        _mod_kernel,
        out_shape=jax.ShapeDtypeStruct((depth, 8, n), F32),
        grid=(depth, n // tn),
        in_specs=[
            pl.BlockSpec((8, d), lambda l, j: (0, 0)),
            pl.BlockSpec((1, d, tn), lambda l, j: (l, 0, j)),
            pl.BlockSpec((1, 1, tn), lambda l, j: (l, 0, j)),
        ],
        out_specs=pl.BlockSpec((1, 8, tn), lambda l, j: (l, 0, j)),
        compiler_params=_cparams(2, _vmem_limit(d * tn * 4, temp_bytes=d * tn * 2)),
        name="adaln_mod",
    )(cin, w_mod, b_mod.reshape(depth, 1, n))


def _rms_mod(x, g, sh, sc):
    ms = jnp.mean(x * x, axis=-1, keepdims=True)
    return (x * lax.rsqrt(ms + EPS) * g) * (1.0 + sc) + sh


def _norm_kernel(x_ref, g_ref, sh_ref, sc_ref, h_ref):
    h_ref[...] = _rms_mod(x_ref[...], g_ref[...], sh_ref[0], sc_ref[0]).astype(BF16)


def _pack_bf16_pairs(hf):
    n = hf.shape[1] // 2
    hi = pltpu.bitcast(hf[:, :n].astype(BF16).astype(F32), jnp.uint32)
    lo = pltpu.bitcast(hf[:, n:].astype(BF16).astype(F32), jnp.uint32)
    return hi | (lo >> 16)


def _unpack_bf16_pairs(p):
    hi = pltpu.bitcast(p & jnp.uint32(0xFFFF0000), F32)
    lo = pltpu.bitcast(p << 16, F32)
    return hi, lo


def _resnorm_kernel(x_ref, y_ref, gt_ref, g_ref, sh_ref, sc_ref, xo_ref, h_ref):
    x = x_ref[...] + gt_ref[0] * y_ref[...].astype(F32)
    xo_ref[...] = x
    h_ref[...] = _rms_mod(x, g_ref[...], sh_ref[0], sc_ref[0]).astype(BF16)


def _resnorm_route_kernel(x_ref, y_ref, gt_ref, g_ref, sh_ref, sc_ref, wh_ref, wl_ref, br_ref,
                          xo_ref, hp_ref, rt_ref):
    x = x_ref[...] + gt_ref[0] * y_ref[...].astype(F32)
    xo_ref[...] = x
    hf = _rms_mod(x, g_ref[...], sh_ref[0], sc_ref[0])
    hp_ref[...] = _pack_bf16_pairs(hf)
    h_hi = hf.astype(BF16)
    h_lo = (hf - h_hi.astype(F32)).astype(BF16)
    logits = _dot(h_hi, wh_ref[...]) + _dot(h_lo, wh_ref[...]) + _dot(h_hi, wl_ref[...]) + br_ref[...]
    lane = lax.broadcasted_iota(jnp.int32, logits.shape, 1)
    m1 = jnp.max(logits, axis=-1, keepdims=True)
    i1 = jnp.min(jnp.where(logits == m1, lane, LANE), axis=-1, keepdims=True)
    rest = jnp.where(lane == i1, NEG_BIG * 2, logits)
    m2 = jnp.max(rest, axis=-1, keepdims=True)
    i2 = jnp.min(jnp.where(rest == m2, lane, LANE), axis=-1, keepdims=True)
    e = jnp.exp(m2 - m1)
    w1 = 1.0 / (1.0 + e)
    w2 = e * w1
    rt = jnp.where(lane == 0, i1.astype(F32),
                   jnp.where(lane == 1, i2.astype(F32),
                             jnp.where(lane == 2, w1, jnp.where(lane == 3, w2, 0.0))))
    rt_ref[...] = rt


def _mod_row(t, tiles_per_batch, ctx_tiles):
    return jnp.where(t % tiles_per_batch < ctx_tiles, 2, t // tiles_per_batch)


def _mod_spec(d, layer, which, tiles_per_batch, ctx_tiles):
    def imap(t):
        return ((layer * 8 + _mod_row(t, tiles_per_batch, ctx_tiles)) * 6 + which, 0, 0)
    return pl.BlockSpec((1, 1, d), imap)


def _norm_call(x, g, mods3, layer, which_sh, which_sc, geo):
    r, d = x.shape
    tpb, ct = geo["tiles_per_batch"], geo["ctx_tiles"]
    tr = ROW_TILE
    return pl.pallas_call(
        _norm_kernel,
        out_shape=jax.ShapeDtypeStruct((r, d), BF16),
        grid=(r // tr,),
        in_specs=[
            pl.BlockSpec((tr, d), lambda t: (t, 0)),
            pl.BlockSpec((1, d), lambda t: (0, 0)),
            _mod_spec(d, layer, which_sh, tpb, ct),
            _mod_spec(d, layer, which_sc, tpb, ct),
        ],
        out_specs=pl.BlockSpec((tr, d), lambda t: (t, 0)),
        compiler_params=_cparams(1, _vmem_limit(tr * d * 6, temp_bytes=tr * d * 8)),
        name="norm_mod",
    )(x, g.reshape(1, d), mods3, mods3)


def _resnorm_call(x, y, g, mods3, layer_gate, which_gate, layer_mod, which_sh, which_sc, geo):
    r, d = x.shape
    tpb, ct = geo["tiles_per_batch"], geo["ctx_tiles"]
    tr = ROW_TILE
    return pl.pallas_call(
        _resnorm_kernel,
        out_shape=(jax.ShapeDtypeStruct((r, d), F32), jax.ShapeDtypeStruct((r, d), BF16)),
        grid=(r // tr,),
        in_specs=[
            pl.BlockSpec((tr, d), lambda t: (t, 0)),
            pl.BlockSpec((tr, d), lambda t: (t, 0)),
            _mod_spec(d, layer_gate, which_gate, tpb, ct),
            pl.BlockSpec((1, d), lambda t: (0, 0)),
            _mod_spec(d, layer_mod, which_sh, tpb, ct),
            _mod_spec(d, layer_mod, which_sc, tpb, ct),
        ],
        out_specs=(pl.BlockSpec((tr, d), lambda t: (t, 0)), pl.BlockSpec((tr, d), lambda t: (t, 0))),
        compiler_params=_cparams(1, _vmem_limit(tr * d * 12, temp_bytes=tr * d * 8)),
        name="residual_norm_mod",
    )(x, y, mods3, g.reshape(1, d), mods3, mods3)


def _resnorm_route_call(x, y, g, mods3, layer, w_router, b_router, geo):
    r, d = x.shape
    n_e = w_router.shape[1]
    tpb, ct = geo["tiles_per_batch"], geo["ctx_tiles"]
    tr = ROW_TILE
    wpad = jnp.zeros((d, LANE), F32).at[:, :n_e].set(w_router)
    w_hi = wpad.astype(BF16)
    w_lo = (wpad - w_hi.astype(F32)).astype(BF16)
    bpad = jnp.full((1, LANE), NEG_BIG, F32).at[0, :n_e].set(b_router)
    return pl.pallas_call(
        _resnorm_route_kernel,
        out_shape=(jax.ShapeDtypeStruct((r, d), F32), jax.ShapeDtypeStruct((r, d // 2), jnp.uint32),
                   jax.ShapeDtypeStruct((r, LANE), F32)),
        grid=(r // tr,),
        in_specs=[
            pl.BlockSpec((tr, d), lambda t: (t, 0)),
            pl.BlockSpec((tr, d), lambda t: (t, 0)),
            _mod_spec(d, layer, 2, tpb, ct),
            pl.BlockSpec((1, d), lambda t: (0, 0)),
            _mod_spec(d, layer, 3, tpb, ct),
            _mod_spec(d, layer, 4, tpb, ct),
            pl.BlockSpec((d, LANE), lambda t: (0, 0)),
            pl.BlockSpec((d, LANE), lambda t: (0, 0)),
            pl.BlockSpec((1, LANE), lambda t: (0, 0)),
        ],
        out_specs=(pl.BlockSpec((tr, d), lambda t: (t, 0)), pl.BlockSpec((tr, d // 2), lambda t: (t, 0)),
                   pl.BlockSpec((tr, LANE), lambda t: (t, 0))),
        compiler_params=_cparams(1, _vmem_limit(tr * d * 12 + 2 * d * LANE * 2, temp_bytes=tr * d * 12)),
        name="residual_norm_route",
    )(x, y, mods3, g.reshape(1, d), mods3, mods3, w_hi, w_lo, bpad)


def _mm_kernel(x_ref, w_ref, o_ref):
    o_ref[...] = _dot(x_ref[...], w_ref[...]).astype(o_ref.dtype)


def _mm_call(x, w, col0, n_cols, tm, tn, name):
    r, k = x.shape
    c0 = col0 // tn
    return pl.pallas_call(
        _mm_kernel,
        out_shape=jax.ShapeDtypeStruct((r, n_cols), BF16),
        grid=(r // tm, n_cols // tn),
        in_specs=[
            pl.BlockSpec((tm, k), lambda i, j: (i, 0)),
            pl.BlockSpec((k, tn), lambda i, j: (0, c0 + j)),
        ],
        out_specs=pl.BlockSpec((tm, tn), lambda i, j: (i, j)),
        compiler_params=_cparams(2, _vmem_limit(tm * k * 2 + k * tn * 2 + tm * tn * 2, temp_bytes=tm * tn * 4)),
        name=name,
    )(x, w)


def _glu_kernel(x_ref, wa_ref, wg_ref, o_ref):
    x = x_ref[...]
    a = _dot(x, wa_ref[...])
    g = _dot(x, wg_ref[...])
    o_ref[...] = (a * _sigmoid(g)).astype(o_ref.dtype)


def _glu_call(x, w, col_a, col_g, n_cols, tm, tn):
    r, k = x.shape
    ca, cg = col_a // tn, col_g // tn
    return pl.pallas_call(
        _glu_kernel,
        out_shape=jax.ShapeDtypeStruct((r, n_cols), BF16),
        grid=(r // tm, n_cols // tn),
        in_specs=[
            pl.BlockSpec((tm, k), lambda i, j: (i, 0)),
            pl.BlockSpec((k, tn), lambda i, j: (0, ca + j)),
            pl.BlockSpec((k, tn), lambda i, j: (0, cg + j)),
        ],
        out_specs=pl.BlockSpec((tm, tn), lambda i, j: (i, j)),
        compiler_params=_cparams(2, _vmem_limit(tm * k * 2 + 2 * k * tn * 2 + tm * tn * 2, temp_bytes=3 * tm * tn * 4)),
        name="conv_glu_proj",
    )(x, w, w)


def _gate_kernel(x_ref, w_ref, b_ref, o_ref):
    o_ref[...] = _sigmoid(_dot(x_ref[...], w_ref[...]) + b_ref[...]).astype(o_ref.dtype)


def _gate_call(x, w, col0, bias, tm, tn):
    r, k = x.shape
    n_cols = bias.shape[0]
    c0 = col0 // tn
    return pl.pallas_call(
        _gate_kernel,
        out_shape=jax.ShapeDtypeStruct((r, n_cols), BF16),
        grid=(r // tm, n_cols // tn),
        in_specs=[
            pl.BlockSpec((tm, k), lambda i, j: (i, 0)),
            pl.BlockSpec((k, tn), lambda i, j: (0, c0 + j)),
            pl.BlockSpec((1, tn), lambda i, j: (0, j)),
        ],
        out_specs=pl.BlockSpec((tm, tn), lambda i, j: (i, j)),
        compiler_params=_cparams(2, _vmem_limit(tm * k * 2 + k * tn * 2 + tm * tn * 2, temp_bytes=2 * tm * tn * 4)),
        name="branch_gates",
    )(x, w, bias.reshape(1, n_cols))


def _swap16(x):
    lane = lax.broadcasted_iota(jnp.int32, x.shape, 1)
    return jnp.where(lane % 32 < 16, pltpu.roll(x, LANE - 16, 1), pltpu.roll(x, 16, 1))


def _rms_rows(x, g):
    ms = jnp.mean(x * x, axis=-1, keepdims=True)
    return x * lax.rsqrt(ms + EPS) * g


def _q_kernel(cq_ref, lg_ref, w_ref, ng_ref, cos_ref, sin_ref, o_ref, xn_ref):
    @pl.when(pl.program_id(1) == 0)
    def _():
        xn_ref[...] = _rms_rows(cq_ref[...].astype(F32), lg_ref[...]).astype(BF16)

    q = _dot(xn_ref[...], w_ref[...])
    ms = jnp.sum(q * q, axis=-1, keepdims=True) * (1.0 / QK_HEAD)
    qn = q * lax.rsqrt(ms + EPS) * ng_ref[...]
    qr = qn[:, QK_NOPE:]
    qr = qr * cos_ref[...] + _swap16(qr) * sin_ref[...]
    o_ref[:, :QK_NOPE] = (qn[:, :QK_NOPE] * SOFTMAX_SCALE).astype(BF16)
    o_ref[:, QK_NOPE:] = (qr * SOFTMAX_SCALE).astype(BF16)


def _q_call(pm, lora_g, w_uq_p, norm_g_p, cos_t, sin_t, tm, geo):
    r = pm.shape[0]
    kq = lora_g.shape[0]
    n_heads = w_uq_p.shape[1] // MXU_DIM
    tiles_b = geo["rows_per_batch"] // tm
    return pl.pallas_call(
        _q_kernel,
        out_shape=jax.ShapeDtypeStruct((r, n_heads * MXU_DIM), BF16),
        grid=(r // tm, n_heads),
        in_specs=[
            pl.BlockSpec((tm, kq), lambda i, h: (i, 0)),
            pl.BlockSpec((1, kq), lambda i, h: (0, 0)),
            pl.BlockSpec((kq, MXU_DIM), lambda i, h: (0, h)),
            pl.BlockSpec((1, MXU_DIM), lambda i, h: (0, 0)),
            pl.BlockSpec((tm, LANE), lambda i, h: (i % tiles_b, 0)),
            pl.BlockSpec((tm, LANE), lambda i, h: (i % tiles_b, 0)),
        ],
        out_specs=pl.BlockSpec((tm, MXU_DIM), lambda i, h: (i, h)),
        scratch_shapes=[pltpu.VMEM((tm, kq), BF16)],
        compiler_params=_cparams(2, _vmem_limit(tm * kq * 2 + kq * MXU_DIM * 2 + 2 * tm * LANE * 4 + tm * MXU_DIM * 2,
                                                scratch_bytes=tm * kq * 2, temp_bytes=tm * kq * 8)),
        name="mla_q",
    )(pm, lora_g.reshape(1, kq), w_uq_p, norm_g_p, cos_t, sin_t)


def _kv_kernel(ckv_ref, kr_ref, lg_ref, w_ref, gn_ref, gr_ref, cos_ref, sin_ref, k_ref, v_ref, xn_ref):
    @pl.when(pl.program_id(1) == 0)
    def _():
        xn_ref[...] = _rms_rows(ckv_ref[...].astype(F32), lg_ref[...]).astype(BF16)

    kv = _dot(xn_ref[...], w_ref[...])
    kn = kv[:, :QK_NOPE]
    kr = kr_ref[...].astype(F32)
    ss = jnp.sum(kn * kn, axis=-1, keepdims=True) + jnp.sum(kr * kr, axis=-1, keepdims=True)
    rs = lax.rsqrt(ss * (1.0 / QK_HEAD) + EPS)
    krn = kr * rs * gr_ref[...]
    krn = krn * cos_ref[...] + _swap16(krn) * sin_ref[...]
    k_ref[:, :QK_NOPE] = (kn * rs * gn_ref[...]).astype(BF16)
    k_ref[:, QK_NOPE:] = krn.astype(BF16)
    v_ref[...] = kv[:, QK_NOPE:].astype(BF16)


def _kv_call(pm, col_ckv, col_kr, lora_g, w_ukv, gn, gr, cos_t, sin_t, tm, geo):
    r = pm.shape[0]
    kk = lora_g.shape[0]
    n_heads = w_ukv.shape[1] // (QK_NOPE + V_HEAD)
    tiles_b = geo["rows_per_batch"] // tm
    c_ckv, c_kr = col_ckv // kk, col_kr // LANE
    return pl.pallas_call(
        _kv_kernel,
        out_shape=(jax.ShapeDtypeStruct((r, n_heads * MXU_DIM), BF16),
                   jax.ShapeDtypeStruct((r, n_heads * V_HEAD), BF16)),
        grid=(r // tm, n_heads),
        in_specs=[
            pl.BlockSpec((tm, kk), lambda i, h: (i, c_ckv)),
            pl.BlockSpec((tm, LANE), lambda i, h: (i, c_kr)),
            pl.BlockSpec((1, kk), lambda i, h: (0, 0)),
            pl.BlockSpec((kk, MXU_DIM), lambda i, h: (0, h)),
            pl.BlockSpec((1, LANE), lambda i, h: (0, 0)),
            pl.BlockSpec((1, LANE), lambda i, h: (0, 0)),
            pl.BlockSpec((tm, LANE), lambda i, h: (i % tiles_b, 0)),
            pl.BlockSpec((tm, LANE), lambda i, h: (i % tiles_b, 0)),
        ],
        out_specs=(pl.BlockSpec((tm, MXU_DIM), lambda i, h: (i, h)),
                   pl.BlockSpec((tm, V_HEAD), lambda i, h: (i, h))),
        scratch_shapes=[pltpu.VMEM((tm, kk), BF16)],
        compiler_params=_cparams(2, _vmem_limit(tm * kk * 2 + tm * LANE * 2 + kk * MXU_DIM * 2 + 2 * tm * LANE * 4
                                                + tm * (MXU_DIM + V_HEAD) * 2,
                                                scratch_bytes=tm * kk * 2, temp_bytes=tm * kk * 8)),
        name="mla_kv",
    )(pm, pm, lora_g.reshape(1, kk), w_ukv, gn, gr, cos_t, sin_t)


def _softmax_pv(s, v):
    m = jnp.max(s, axis=-1, keepdims=True)
    p = jnp.exp(s - m)
    l = jnp.sum(p, axis=-1, keepdims=True)
    return _dot(p.astype(BF16), v) / l


def _attn_kernel(q_ref, k_ref, v_ref, o_ref, *, n_chunks, chunk, ctx_len, fix_ctx):
    q = q_ref[...]
    tq = q.shape[0]
    m = jnp.full((tq, 1), NEG_BIG, F32)
    l = jnp.zeros((tq, 1), F32)
    acc = jnp.zeros((tq, V_HEAD), F32)
    for c in range(n_chunks):
        k = k_ref[c * chunk:(c + 1) * chunk, :]
        v = v_ref[c * chunk:(c + 1) * chunk, :]
        s = lax.dot_general(q, k, (((1,), (1,)), ((), ())), preferred_element_type=F32)
        m_new = jnp.maximum(m, jnp.max(s, axis=-1, keepdims=True))
        alpha = jnp.exp(m - m_new)
        p = jnp.exp(s - m_new)
        l = alpha * l + jnp.sum(p, axis=-1, keepdims=True)
        acc = alpha * acc + _dot(p.astype(BF16), v)
        m = m_new
    o_ref[...] = (acc / l).astype(o_ref.dtype)

    if fix_ctx:
        @pl.when(pl.program_id(2) == 0)
        def _():
            qc = q_ref[:ctx_len, :]
            s = lax.dot_general(qc, k_ref[:ctx_len, :], (((1,), (1,)), ((), ())), preferred_element_type=F32)
            o_ref[:ctx_len, :] = _softmax_pv(s, v_ref[:ctx_len, :]).astype(o_ref.dtype)


def _attn_call(q, k, v, batch, tq, geo, fix_ctx):
    r = q.shape[0]
    n_heads = q.shape[1] // MXU_DIM
    rb = geo["rows_per_batch"]
    nq = rb // tq
    chunk = _pick_tile(rb, (1408, 1280, 1024, 768, 640, 512, 256))
    kern = functools.partial(_attn_kernel, n_chunks=rb // chunk, chunk=chunk, ctx_len=geo["ctx_len"], fix_ctx=fix_ctx)
    return pl.pallas_call(
        kern,
        out_shape=jax.ShapeDtypeStruct((r, n_heads * V_HEAD), BF16),
        grid=(batch, n_heads, nq),
        in_specs=[
            pl.BlockSpec((tq, MXU_DIM), lambda b, h, i: (b * nq + i, h)),
            pl.BlockSpec((rb, MXU_DIM), lambda b, h, i: (b, h)),
            pl.BlockSpec((rb, V_HEAD), lambda b, h, i: (b, h)),
        ],
        out_specs=pl.BlockSpec((tq, V_HEAD), lambda b, h, i: (b * nq + i, h)),
        compiler_params=_cparams(3, _vmem_limit(tq * MXU_DIM * 2 + rb * (MXU_DIM + V_HEAD) * 2 + tq * V_HEAD * 2,
                                                temp_bytes=4 * tq * chunk * 4)),
        name="mla_attention",
    )(q, k, v)


def _conv_kernel(u_ref, w_ref, b_ref, o_ref, pad_ref, *, ctx_len, seq_len, n_taps, chunk):
    tc = u_ref.shape[1]
    halo = CONV_HALO
    zeros = jnp.zeros((halo, tc), F32)
    lat0 = ctx_len + 2 * halo
    pad_ref[0:halo, :] = zeros
    pad_ref[halo:halo + ctx_len, :] = u_ref[0:ctx_len, :].astype(F32)
    pad_ref[halo + ctx_len:lat0, :] = zeros
    pad_ref[lat0:lat0 + seq_len, :] = u_ref[ctx_len:ctx_len + seq_len, :].astype(F32)
    pad_ref[lat0 + seq_len:lat0 + seq_len + halo, :] = zeros
    w = w_ref[...]
    bias = b_ref[...]
    first = halo - n_taps // 2

    def run(out0, pad0, n_rows):
        def body(c, carry):
            base = pl.multiple_of(c * chunk, chunk)
            win = pad_ref[pl.ds(pad0 + base, chunk + 2 * halo), :]
            acc = jnp.zeros((chunk, tc), F32)
            for t in range(n_taps):
                acc = acc + win[first + t:first + t + chunk, :] * w[t:t + 1, :]
            o_ref[pl.ds(out0 + base, chunk), :] = (acc + bias).astype(o_ref.dtype)
            return carry
        lax.fori_loop(0, n_rows // chunk, body, 0)

    run(0, 0, ctx_len)
    run(ctx_len, lat0 - halo, seq_len)


def _conv_call(u, w_dw, b_dw, batch, geo):
    r, cw = u.shape
    n_taps = w_dw.shape[0]
    rb, ctx_len, seq_len = geo["rows_per_batch"], geo["ctx_len"], geo["seq_len"]
    tc = _pick_tile(cw, (256, 128))
    chunk = 64
    wp = jnp.zeros((32, cw), F32).at[:n_taps].set(w_dw)
    pad_rows = rb + 3 * CONV_HALO
    kern = functools.partial(_conv_kernel, ctx_len=ctx_len, seq_len=seq_len, n_taps=n_taps, chunk=chunk)
    return pl.pallas_call(
        kern,
        out_shape=jax.ShapeDtypeStruct((r, cw), BF16),
        grid=(batch, cw // tc),
        in_specs=[
            pl.BlockSpec((rb, tc), lambda b, c: (b, c)),
            pl.BlockSpec((32, tc), lambda b, c: (0, c)),
            pl.BlockSpec((1, tc), lambda b, c: (0, c)),
        ],
        out_specs=pl.BlockSpec((rb, tc), lambda b, c: (b, c)),
        scratch_shapes=[pltpu.VMEM((pad_rows, tc), F32)],
        compiler_params=_cparams(2, _vmem_limit(2 * rb * tc * 2 + 33 * tc * 4, scratch_bytes=pad_rows * tc * 4,
                                                temp_bytes=rb * tc * 4)),
        name="conformer_dwconv",
    )(u, wp, b_dw.reshape(1, cw))


def _merge_kernel(a_ref, c_ref, lg_ref, lb_ref, ga_ref, gb_ref, wa_ref, wb_ref, o_ref, cn_ref):
    @pl.when(pl.program_id(1) == 0)
    def _():
        x = c_ref[...].astype(F32)
        mu = jnp.mean(x, axis=-1, keepdims=True)
        xc = x - mu
        var = jnp.mean(xc * xc, axis=-1, keepdims=True)
        y = xc * lax.rsqrt(var + EPS) * lg_ref[...] + lb_ref[...]
        cn_ref[...] = (y * _sigmoid(y)).astype(BF16)

    ya = _dot(a_ref[...], wa_ref[...])
    yb = _dot(cn_ref[...], wb_ref[...])
    o_ref[...] = (ga_ref[...].astype(F32) * ya + gb_ref[...].astype(F32) * yb).astype(o_ref.dtype)


def _merge_call(attn, conv, ln_g, ln_b, gates, w_oa, w_ob, tm, tn):
    r, ka = attn.shape
    kc = conv.shape[1]
    d = w_oa.shape[1]
    nb = d // tn
    return pl.pallas_call(
        _merge_kernel,
        out_shape=jax.ShapeDtypeStruct((r, d), BF16),
        grid=(r // tm, nb),
        in_specs=[
            pl.BlockSpec((tm, ka), lambda i, j: (i, 0)),
            pl.BlockSpec((tm, kc), lambda i, j: (i, 0)),
            pl.BlockSpec((1, kc), lambda i, j: (0, 0)),
            pl.BlockSpec((1, kc), lambda i, j: (0, 0)),
            pl.BlockSpec((tm, tn), lambda i, j: (i, j)),
            pl.BlockSpec((tm, tn), lambda i, j: (i, nb + j)),
            pl.BlockSpec((ka, tn), lambda i, j: (0, j)),
            pl.BlockSpec((kc, tn), lambda i, j: (0, j)),
        ],
        out_specs=pl.BlockSpec((tm, tn), lambda i, j: (i, j)),
        scratch_shapes=[pltpu.VMEM((tm, kc), BF16)],
        compiler_params=_cparams(2, _vmem_limit(tm * (ka + kc) * 2 + 3 * tm * tn * 2 + (ka + kc) * tn * 2,
                                                scratch_bytes=tm * kc * 2, temp_bytes=tm * kc * 8 + 3 * tm * tn * 4)),
        name="branch_merge",
    )(attn, conv, ln_g.reshape(1, kc), ln_b.reshape(1, kc), gates, gates, w_oa, w_ob)


DOWN_COLS = 1024
PACK_COLS = 512


def _swiglu_accumulate(x, wg, wu, wd_ref, acc_ref, first):
    g = _dot(x, wg)
    u = _dot(x, wu)
    a = (g * _sigmoid(g) * u).astype(BF16)

    @pl.when(first)
    def _():
        acc_ref[...] = jnp.zeros(acc_ref.shape, acc_ref.dtype)

    d = acc_ref.shape[1]
    cw = min(DOWN_COLS, d)
    for c0 in range(0, d, cw):
        acc_ref[:, c0:c0 + cw] += _dot(a, wd_ref[:, c0:c0 + cw])


def _ffn_kernel(x_ref, wg_ref, wu_ref, wd_ref, o_ref, acc_ref):
    f = pl.program_id(1)
    _swiglu_accumulate(x_ref[...], wg_ref[...], wu_ref[...], wd_ref, acc_ref, f == 0)

    @pl.when(f == pl.num_programs(1) - 1)
    def _():
        o_ref[...] = acc_ref[...].astype(o_ref.dtype)


def _ffn_call(h, wg, wu, wd, tm, tf):
    r, d = h.shape
    ff = wg.shape[1]
    return pl.pallas_call(
        _ffn_kernel,
        out_shape=jax.ShapeDtypeStruct((r, d), BF16),
        grid=(r // tm, ff // tf),
        in_specs=[
            pl.BlockSpec((tm, d), lambda i, f: (i, 0), pipeline_mode=pl.Buffered(1)),
            pl.BlockSpec((d, tf), lambda i, f: (0, f)),
            pl.BlockSpec((d, tf), lambda i, f: (0, f)),
            pl.BlockSpec((tf, d), lambda i, f: (f, 0)),
        ],
        out_specs=pl.BlockSpec((tm, d), lambda i, f: (i, 0), pipeline_mode=pl.Buffered(1)),
        scratch_shapes=[pltpu.VMEM((tm, d), F32)],
        compiler_params=_cparams(2, _vmem_limit(3 * d * tf * 2 + tm * d * 2, scratch_bytes=tm * d * 4,
                                                temp_bytes=tm * DOWN_COLS * 4 + 3 * tm * tf * 4)),
        name="dense_swiglu",
    )(h, wg, wu, wd)


def _moe_kernel(te_ref, nu_ref, tok_ref, hp_hbm, wg_ref, wu_ref, wd_ref, rw_ref, o_ref, xp_ref, x_ref, acc_ref, sem):
    t = pl.program_id(0)
    f = pl.program_id(1)
    nf = pl.num_programs(1)
    tm = xp_ref.shape[0]
    used = t < nu_ref[0]

    def row_copy(r):
        tok = tok_ref[t * tm + r]
        return pltpu.make_async_copy(hp_hbm.at[pl.ds(tok, 1)], xp_ref.at[pl.ds(r, 1)], sem)

    @pl.when(used & (f == 0))
    def _():
        def issue(r, c):
            row_copy(r).start()
            return c
        lax.fori_loop(0, tm, issue, 0)

        def drain(r, c):
            row_copy(r).wait()
            return c
        lax.fori_loop(0, tm, drain, 0)
        half = xp_ref.shape[1]
        cw = min(PACK_COLS, half)
        for c0 in range(0, half, cw):
            hi, lo = _unpack_bf16_pairs(xp_ref[:, c0:c0 + cw])
            x_ref[:, c0:c0 + cw] = hi.astype(BF16)
            x_ref[:, half + c0:half + c0 + cw] = lo.astype(BF16)

    @pl.when(used)
    def _():
        _swiglu_accumulate(x_ref[...], wg_ref[0], wu_ref[0], wd_ref.at[0], acc_ref, f == 0)

        @pl.when(f == nf - 1)
        def _():
            half = o_ref.shape[1]
            cw = min(PACK_COLS, half)
            rw = rw_ref[...]
            for c0 in range(0, half, cw):
                hi = pltpu.bitcast((acc_ref[:, c0:c0 + cw] * rw).astype(BF16).astype(F32), jnp.uint32)
                lo = pltpu.bitcast((acc_ref[:, half + c0:half + c0 + cw] * rw).astype(BF16).astype(F32), jnp.uint32)
                o_ref[:, c0:c0 + cw] = hi | (lo >> 16)

    @pl.when(jnp.logical_not(used) & (f == nf - 1))
    def _():
        o_ref[...] = jnp.zeros(o_ref.shape, o_ref.dtype)


def _moe_call(hp, tile_expert, n_used, row_tok, row_w, weg, weu, wed, tm, tf):
    n_rows = row_tok.shape[0]
    d2 = hp.shape[1]
    d = 2 * d2
    ffe = weg.shape[2]
    nf = ffe // tf

    def w_in_map(t, f, te, nu, tok):
        return (te[t], 0, jnp.where(t < nu[0], f, nf - 1))

    def w_dn_map(t, f, te, nu, tok):
        return (te[t], jnp.where(t < nu[0], f, nf - 1), 0)

    grid_spec = pltpu.PrefetchScalarGridSpec(
        num_scalar_prefetch=3,
        grid=(n_rows // tm, nf),
        in_specs=[
            pl.BlockSpec(memory_space=pl.ANY),
            pl.BlockSpec((1, d, tf), w_in_map),
            pl.BlockSpec((1, d, tf), w_in_map),
            pl.BlockSpec((1, tf, d), w_dn_map),
            pl.BlockSpec((tm, 1), lambda t, f, te, nu, tok: (t, 0)),
        ],
        out_specs=pl.BlockSpec((tm, d2), lambda t, f, te, nu, tok: (t, 0), pipeline_mode=pl.Buffered(1)),
        scratch_shapes=[pltpu.VMEM((tm, d2), jnp.uint32), pltpu.VMEM((tm, d), BF16), pltpu.VMEM((tm, d), F32),
                        pltpu.SemaphoreType.DMA],
    )
    return pl.pallas_call(
        _moe_kernel,
        out_shape=jax.ShapeDtypeStruct((n_rows, d2), jnp.uint32),
        grid_spec=grid_spec,
        compiler_params=_cparams(2, _vmem_limit(3 * d * tf * 2 + tm * d2 * 2 + tm * LANE * 4,
                                                scratch_bytes=tm * d2 * 4 + tm * d * 2 + tm * d * 4,
                                                temp_bytes=tm * DOWN_COLS * 4 + 3 * tm * tf * 4)),
        name="moe_experts",
    )(tile_expert, n_used, row_tok, hp, weg, weu, wed, row_w)


def _combine_kernel(pos_ref, x_ref, gt_ref, ys_hbm, o_ref, buf_ref, sem, *, n_tok):
    t = pl.program_id(0)
    tr = x_ref.shape[0]

    def row_copy(r, which):
        pos = pos_ref[which * n_tok + t * tr + r]
        return pltpu.make_async_copy(ys_hbm.at[pl.ds(pos, 1)], buf_ref.at[which, pl.ds(r, 1)], sem)

    def issue(r, c):
        row_copy(r, 0).start()
        row_copy(r, 1).start()
        return c
    lax.fori_loop(0, tr, issue, 0)

    def drain(r, c):
        row_copy(r, 0).wait()
        row_copy(r, 1).wait()
        return c
    lax.fori_loop(0, tr, drain, 0)

    h0, l0 = _unpack_bf16_pairs(buf_ref[0])
    h1, l1 = _unpack_bf16_pairs(buf_ref[1])
    half = h0.shape[1]
    g = gt_ref[0]
    o_ref[:, :half] = x_ref[:, :half] + g[:, :half] * (h0 + h1)
    o_ref[:, half:] = x_ref[:, half:] + g[:, half:] * (l0 + l1)


def _combine_call(x, ys, pos, mods3, layer, batch, geo):
    d = x.shape[1]
    tr = ROW_TILE
    seq_len = geo["seq_len"]
    tpb, ct = geo["tiles_per_batch"], geo["ctx_tiles"]
    lat_tiles = seq_len // tr
    n_tok = batch * seq_len

    def x_map(t, pos_ref):
        return ((t // lat_tiles) * tpb + ct + t % lat_tiles, 0)

    def g_map(t, pos_ref):
        return ((layer * 8 + t // lat_tiles) * 6 + 5, 0, 0)

    grid_spec = pltpu.PrefetchScalarGridSpec(
        num_scalar_prefetch=1,
        grid=(n_tok // tr,),
        in_specs=[
            pl.BlockSpec((tr, d), x_map),
            pl.BlockSpec((1, 1, d), g_map),
            pl.BlockSpec(memory_space=pl.ANY),
        ],
        out_specs=pl.BlockSpec((tr, d), lambda t, pos_ref: (t, 0)),
        scratch_shapes=[pltpu.VMEM((2, tr, d // 2), jnp.uint32), pltpu.SemaphoreType.DMA],
    )
    return pl.pallas_call(
        functools.partial(_combine_kernel, n_tok=n_tok),
        out_shape=jax.ShapeDtypeStruct((n_tok, d), F32),
        grid_spec=grid_spec,
        compiler_params=_cparams(1, _vmem_limit(2 * tr * d * 4, scratch_bytes=tr * d * 4, temp_bytes=2 * tr * d * 4)),
        name="moe_combine",
    )(pos, x, mods3, ys)


def _route_plan(route, n_experts, tm):
    n_tok = route.shape[0]
    e = jnp.concatenate([route[:, 0], route[:, 1]]).astype(jnp.int32)
    w = jnp.concatenate([route[:, 2], route[:, 3]])
    tok = jnp.concatenate([jnp.arange(n_tok, dtype=jnp.int32)] * 2)
    onehot = (e[:, None] == jnp.arange(n_experts, dtype=jnp.int32)[None, :]).astype(jnp.int32)
    csum = jnp.cumsum(onehot, axis=0)
    rank = jnp.sum(onehot * (csum - 1), axis=1)
    counts = csum[-1]
    padded = ((counts + tm - 1) // tm) * tm
    ends = jnp.cumsum(padded)
    starts = ends - padded
    pos = jnp.sum(onehot * starts[None, :], axis=1) + rank
    n_rows = ((2 * n_tok + n_experts * (tm - 1) + tm - 1) // tm) * tm
    row_tok = jnp.zeros((n_rows,), jnp.int32).at[pos].set(tok)
    row_w = jnp.zeros((n_rows,), F32).at[pos].set(w)
    n_used = (ends[-1] // tm).astype(jnp.int32)
    tile_start = jnp.arange(n_rows // tm, dtype=jnp.int32) * tm
    tile_expert = jnp.sum((tile_start[:, None] >= ends[None, :]).astype(jnp.int32), axis=1)
    last_e = jnp.sum((jnp.maximum(ends[-1] - 1, 0) >= ends).astype(jnp.int32))
    tile_expert = jnp.minimum(jnp.where(tile_start < ends[-1], tile_expert, last_e), n_experts - 1).astype(jnp.int32)
    return tile_expert, n_used.reshape(1), row_tok, row_w.reshape(n_rows, 1), pos.astype(jnp.int32)


def _rope_tables(geo):
    ctx_len, seq_len = geo["ctx_len"], geo["seq_len"]
    quarter = QK_ROPE // 4
    pos = jnp.arange(seq_len)
    rows = (pos // GRID_W).astype(F32)
    cols = (pos % GRID_W).astype(F32)
    inv = ROPE_BASE ** (-jnp.arange(quarter, dtype=F32) / quarter)
    ang = jnp.concatenate([rows[:, None] * inv] * 2 + [cols[:, None] * inv] * 2, axis=1)
    sign = jnp.tile(jnp.concatenate([-jnp.ones((quarter,), F32), jnp.ones((quarter,), F32)]), 2)
    cos = jnp.concatenate([jnp.cos(ang), jnp.ones((seq_len, LANE - QK_ROPE), F32)], axis=1)
    sin = jnp.concatenate([jnp.sin(ang) * sign, jnp.zeros((seq_len, LANE - QK_ROPE), F32)], axis=1)
    cos = jnp.concatenate([jnp.ones((ctx_len, LANE), F32), cos], axis=0)
    sin = jnp.concatenate([jnp.zeros((ctx_len, LANE), F32), sin], axis=0)
    return cos, sin


def kernel(x, c, ctx, c_ctx, w_mod, b_mod, norm1_g, norm2_g, w_in, b_gate, q_lora_g, w_uq, kv_lora_g, w_ukv,
           q_norm_g, k_norm_g, w_oa, w_dw, b_dw, conv_ln_g, conv_ln_b, w_ob, w_out, w_ff_gate, w_ff_up, w_ff_down,
           w_router, b_router, w_e_gate, w_e_up, w_e_down):
    batch, seq_len, d = x.shape
    ctx_len = ctx.shape[1]
    depth = w_mod.shape[0]
    q_lora, kv_lora = q_lora_g.shape[1], kv_lora_g.shape[1]
    n_heads = w_uq.shape[2] // QK_HEAD
    conv_w = w_dw.shape[2]
    n_experts = w_router.shape[2]
    mla_in = q_lora + kv_lora + QK_ROPE
    assert depth == 2 and batch == 2, "layer plan (dense layer with context, then routed layer) is written for depth 2"
    assert ctx_len % ROW_TILE == 0 and seq_len % ROW_TILE == 0 and seq_len % GRID_W == 0
    assert w_ukv.shape[2] == n_heads * (QK_NOPE + V_HEAD) and QK_NOPE + V_HEAD == MXU_DIM
    assert q_lora % LANE == 0 and kv_lora % LANE == 0 and (q_lora % kv_lora == 0)

    rb = ctx_len + seq_len
    r = batch * rb
    geo = dict(rows_per_batch=rb, ctx_len=ctx_len, seq_len=seq_len,
               tiles_per_batch=rb // ROW_TILE, ctx_tiles=ctx_len // ROW_TILE)
    tm = _pick_tile(rb, (768, 512, 256))
    tn = _pick_tile(conv_w, (512, 256, 128))

    mla_pad = -(-(mla_in + LANE - QK_ROPE) // 1024) * 1024
    col_ckv, col_kr = q_lora, q_lora + kv_lora

    def in_weight(l):
        wl = w_in[l]
        mla = jnp.zeros((d, mla_pad), BF16).at[:, :mla_in].set(wl[:, :mla_in].astype(BF16))
        return jnp.concatenate([mla, wl[:, mla_in:].astype(BF16)], axis=1)
    col_glu_a, col_glu_g, col_gate = mla_pad, mla_pad + conv_w, mla_pad + 2 * conv_w

    def uq_weight(l):
        w3 = w_uq[l].reshape(q_lora, n_heads, QK_HEAD).astype(BF16)
        return jnp.pad(w3, ((0, 0), (0, 0), (0, MXU_DIM - QK_HEAD))).reshape(q_lora, n_heads * MXU_DIM)

    cos_t, sin_t = _rope_tables(geo)

    cin = jnp.zeros((8, d), F32).at[:batch].set(c).at[batch].set(c_ctx)
    mods = _mod_call(cin, w_mod, b_mod)
    mods3 = mods.reshape(depth * 8 * 6, 1, d)

    xcat = jnp.concatenate([ctx, x], axis=1).reshape(r, d)
    h = _norm_call(xcat, norm1_g[0], mods3, 0, 0, 1, geo)

    out = None
    for l in range(depth):
        w1 = in_weight(l)
        pm = _mm_call(h, w1, 0, mla_pad, tm, 1024, "mla_in_proj")
        u = _glu_call(h, w1, col_glu_a, col_glu_g, conv_w, tm, tn)
        gates = _gate_call(h, w1, col_gate, b_gate[l], tm, tn)

        qng = jnp.zeros((1, MXU_DIM), F32).at[0, :QK_HEAD].set(q_norm_g[l])
        q = _q_call(pm, q_lora_g[l], uq_weight(l), qng, cos_t, sin_t, tm, geo)
        kng = k_norm_g[l][:QK_NOPE].reshape(1, QK_NOPE)
        krg = jnp.zeros((1, LANE), F32).at[0, :QK_ROPE].set(k_norm_g[l][QK_NOPE:])
        k, v = _kv_call(pm, col_ckv, col_kr, kv_lora_g[l], w_ukv[l].astype(BF16), kng, krg, cos_t, sin_t, tm, geo)
        attn = _attn_call(q, k, v, batch, tm, geo, fix_ctx=(l == 0))

        conv = _conv_call(u, w_dw[l], b_dw[l], batch, geo)
        m = _merge_call(attn, conv, conv_ln_g[l], conv_ln_b[l], gates, w_oa[l].astype(BF16), w_ob[l].astype(BF16), tm, tn)
        y = _mm_call(m, w_out[l].astype(BF16), 0, d, tm, tn, "out_proj")

        if l == 0:
            x1, h2 = _resnorm_call(xcat, y, norm2_g[l], mods3, l, 2, l, 3, 4, geo)
            ff = w_ff_gate.shape[2]
            tf = _pick_tile(ff, (256, 128))
            y2 = _ffn_call(h2, w_ff_gate[0].astype(BF16), w_ff_up[0].astype(BF16), w_ff_down[0].astype(BF16), tm, tf)
            xcat, h = _resnorm_call(x1, y2, norm1_g[1], mods3, 0, 5, 1, 0, 1, geo)
        else:
            x1, hp, route = _resnorm_route_call(xcat, y, norm2_g[l], mods3, l, w_router[0], b_router[0], geo)
            route_lat = route.reshape(batch, rb, LANE)[:, ctx_len:, :].reshape(batch * seq_len, LANE)
            tme = 512
            tile_expert, n_used, row_tok, row_w, pos = _route_plan(route_lat, n_experts, tme)
            row_tok = (row_tok // seq_len) * rb + ctx_len + row_tok % seq_len
            ffe = w_e_gate.shape[3]
            tfe = _pick_tile(ffe, (256, 128))
            ys = _moe_call(hp, tile_expert, n_used, row_tok, row_w, w_e_gate[0].astype(BF16), w_e_up[0].astype(BF16),
                           w_e_down[0].astype(BF16), tme, tfe)
            out = _combine_call(x1, ys, pos, mods3, l, batch, geo)
    return out.reshape(batch, seq_len, d)
```
